```python
import jax, jax.numpy as jnp
from jax import lax
import numpy as np

D_MODEL = 1024
BATCH = 16
SEQ = 2048
DEPTH = 4

N_MIXERS = 2
CONV_CH = D_MODEL
CONV_KERNEL = 31
GMLP_CHUNK = 128
GMLP_GROUPS = 8
GMLP_DFF = 2 * D_MODEL
GMLP_HALF = GMLP_DFF // 2
GMLP_GROUP_CH = GMLP_HALF // GMLP_GROUPS
FFN_HIDDEN = 2816
FFN_CONV = 3
DEEPNORM_ALPHA = (2.0 * DEPTH) ** 0.25
DEEPNORM_BETA = (8.0 * DEPTH) ** -0.25
LN_EPS = 1e-5
N_CONV_LAYERS = (DEPTH + 1) // 2
N_GMLP_LAYERS = DEPTH // 2

kernel_name = "hybrid_conformer_gmlp_deepnorm"


def layer_norm(x, g, b):
    xf = x.astype(jnp.float32)
    mu = jnp.mean(xf, axis=-1, keepdims=True)
    xc = xf - mu
    var = jnp.mean(xc * xc, axis=-1, keepdims=True)
    y = xc * lax.rsqrt(var + LN_EPS)
    return (y * g.astype(jnp.float32) + b.astype(jnp.float32)).astype(x.dtype)


def causal_depthwise_conv(x, w, b):
    k, c = w.shape
    y = lax.conv_general_dilated(
        x, w[:, None, :].astype(x.dtype), window_strides=(1,), padding=[(k - 1, 0)],
        dimension_numbers=("NWC", "WIO", "NWC"), feature_group_count=c)
    return y + b


def conformer_conv_module(x, w_in, b_in, w_dw, b_dw, ln_g, ln_b, w_out, b_out):
    h = x @ w_in + b_in
    a, gate = jnp.split(h, 2, axis=-1)
    h = a * jax.nn.sigmoid(gate)
    h = causal_depthwise_conv(h, w_dw, b_dw)
    h = jax.nn.silu(layer_norm(h, ln_g, ln_b))
    return h @ w_out + b_out


def chunked_spatial_gating(x, w_in, b_in, ln_g, ln_b, w_s, b_s, w_out, b_out):
    bsz, t, _ = x.shape
    z = jax.nn.gelu(x @ w_in + b_in, approximate=False)
    u, v = jnp.split(z, 2, axis=-1)
    v = layer_norm(v, ln_g, ln_b)
    n_chunks = t // GMLP_CHUNK
    v = v.reshape(bsz, n_chunks, GMLP_CHUNK, GMLP_GROUPS, GMLP_GROUP_CH)
    mask = jnp.tril(jnp.ones((GMLP_CHUNK, GMLP_CHUNK), dtype=bool))
    w_causal = jnp.where(mask[None], w_s, jnp.zeros((), w_s.dtype))
    s = jnp.einsum("gts,bnsgc->bntgc", w_causal, v) + b_s.T[:, :, None]
    s = s.reshape(bsz, t, GMLP_HALF)
    return (u * s) @ w_out + b_out


def conv_ffn(x, w_up, b_up, w_dw, b_dw, w_down, b_down):
    h = x @ w_up + b_up
    h = causal_depthwise_conv(h, w_dw, b_dw)
    g, val = jnp.split(h, 2, axis=-1)
    return (jax.nn.silu(g) * val) @ w_down + b_down


def setup_inputs(seed: int = 0) -> dict:
    key = jax.random.key(seed)
    ks = iter(jax.random.split(key, 32))

    def nrm(shape, std):
        return jax.random.normal(next(ks), shape, jnp.float32) * std

    def xavier_out(shape):
        fan_in, fan_out = shape[-2], shape[-1]
        return nrm(shape, DEEPNORM_BETA * (2.0 / (fan_in + fan_out)) ** 0.5)

    na, nb, d = N_CONV_LAYERS, N_GMLP_LAYERS, D_MODEL
    inp = {}
    inp["x"] = nrm((BATCH, SEQ, d), 1.0)
    inp["conv_w_in"] = nrm((na, d, 2 * CONV_CH), d ** -0.5)
    inp["conv_b_in"] = nrm((na, 2 * CONV_CH), 0.02)
    inp["conv_w_dw"] = nrm((na, CONV_KERNEL, CONV_CH), CONV_KERNEL ** -0.5)
    inp["conv_b_dw"] = nrm((na, CONV_CH), 0.02)
    inp["conv_ln_g"] = 1.0 + nrm((na, CONV_CH), 0.05)
    inp["conv_ln_b"] = nrm((na, CONV_CH), 0.02)
    inp["conv_w_out"] = xavier_out((na, CONV_CH, d))
    inp["conv_b_out"] = nrm((na, d), 0.02)
    inp["gmlp_w_in"] = nrm((nb, d, GMLP_DFF), d ** -0.5)
    inp["gmlp_b_in"] = nrm((nb, GMLP_DFF), 0.02)
    inp["gmlp_ln_g"] = 1.0 + nrm((nb, GMLP_HALF), 0.05)
    inp["gmlp_ln_b"] = nrm((nb, GMLP_HALF), 0.02)
    inp["gmlp_w_s"] = nrm((nb, GMLP_GROUPS, GMLP_CHUNK, GMLP_CHUNK), 0.5 * GMLP_CHUNK ** -0.5)
    inp["gmlp_b_s"] = 1.0 + nrm((nb, GMLP_GROUPS, GMLP_CHUNK), 0.1)
    inp["gmlp_w_out"] = xavier_out((nb, GMLP_HALF, d))
    inp["gmlp_b_out"] = nrm((nb, d), 0.02)
    inp["ffn_w_up"] = nrm((DEPTH, d, 2 * FFN_HIDDEN), d ** -0.5)
    inp["ffn_b_up"] = nrm((DEPTH, 2 * FFN_HIDDEN), 0.02)
    inp["ffn_w_dw"] = nrm((DEPTH, FFN_CONV, 2 * FFN_HIDDEN), FFN_CONV ** -0.5)
    inp["ffn_b_dw"] = nrm((DEPTH, 2 * FFN_HIDDEN), 0.02)
    inp["ffn_w_down"] = xavier_out((DEPTH, FFN_HIDDEN, d))
    inp["ffn_b_down"] = nrm((DEPTH, d), 0.02)
    inp["norm1_g"] = 1.0 + nrm((DEPTH, d), 0.05)
    inp["norm1_b"] = nrm((DEPTH, d), 0.02)
    inp["norm2_g"] = 1.0 + nrm((DEPTH, d), 0.05)
    inp["norm2_b"] = nrm((DEPTH, d), 0.02)
    return inp


def reference(x, conv_w_in, conv_b_in, conv_w_dw, conv_b_dw, conv_ln_g, conv_ln_b,
              conv_w_out, conv_b_out, gmlp_w_in, gmlp_b_in, gmlp_ln_g, gmlp_ln_b,
              gmlp_w_s, gmlp_b_s, gmlp_w_out, gmlp_b_out, ffn_w_up, ffn_b_up,
              ffn_w_dw, ffn_b_dw, ffn_w_down, ffn_b_down, norm1_g, norm1_b,
              norm2_g, norm2_b):
    for i in range(DEPTH):
        j = i // N_MIXERS
        if i % N_MIXERS == 0:
            y = conformer_conv_module(x, conv_w_in[j], conv_b_in[j], conv_w_dw[j], conv_b_dw[j],
                                      conv_ln_g[j], conv_ln_b[j], conv_w_out[j], conv_b_out[j])
        else:
            y = chunked_spatial_gating(x, gmlp_w_in[j], gmlp_b_in[j], gmlp_ln_g[j], gmlp_ln_b[j],
                                       gmlp_w_s[j], gmlp_b_s[j], gmlp_w_out[j], gmlp_b_out[j])
        x = layer_norm(DEEPNORM_ALPHA * x + y, norm1_g[i], norm1_b[i])
        y = conv_ffn(x, ffn_w_up[i], ffn_b_up[i], ffn_w_dw[i], ffn_b_dw[i],
                     ffn_w_down[i], ffn_b_down[i])
        x = layer_norm(DEEPNORM_ALPHA * x + y, norm2_g[i], norm2_b[i])
    return x
```

```python
import functools

import jax
import jax.numpy as jnp
from jax import lax
from jax.experimental import pallas as pl
from jax.experimental.pallas import tpu as pltpu

D_MODEL = 1024
DEPTH = 4
CONV_KERNEL = 31
GMLP_CHUNK = 128
GMLP_GROUPS = 8
FFN_HIDDEN = 2816
FFN_CONV = 3
ALPHA = (2.0 * DEPTH) ** 0.25
LN_EPS = 1e-5

TIME_TILE = 512
FFN_CHUNK = 256
CONV_HALO = 32
FFN_HALO = 8
CONV_ROWS = 64
LANES = 128
VMEM_LIMIT_BYTES = 56 * 1024 * 1024

_BF16 = jnp.bfloat16
_F32 = jnp.float32


def _layer_norm(x, g, b):
    mu = jnp.mean(x, axis=-1, keepdims=True)
    xc = x - mu
    var = jnp.mean(xc * xc, axis=-1, keepdims=True)
    return xc * lax.rsqrt(var + LN_EPS) * g + b


def _dot(a, b):
    return jnp.dot(a, b, preferred_element_type=_F32)


def _resident(shape):
    nd = len(shape)
    return pl.BlockSpec(shape, lambda b, j: (0,) * nd, pipeline_mode=pl.Buffered(1))


def _tile_spec(tm):
    return pl.BlockSpec((1, tm, D_MODEL), lambda b, j: (b, j, 0))


def _params():
    return pltpu.CompilerParams(
        dimension_semantics=("arbitrary", "arbitrary"),
        vmem_limit_bytes=VMEM_LIMIT_BYTES,
    )


def _ffn_kernel(x_ref, wup_ref, bup_ref, wdw_ref, bdw_ref, wdn_ref, bdn_ref,
                g_ref, b_ref, o_ref, hbuf, acc_ref):
    tm = x_ref.shape[1]
    n_chunks = FFN_HIDDEN // FFN_CHUNK
    j = pl.program_id(1)

    @pl.when(j == 0)
    def _():
        hbuf[:, 0:FFN_HALO, :] = jnp.zeros((hbuf.shape[0], FFN_HALO, LANES), _F32)

    x = x_ref[0]
    xb = x.astype(_BF16)
    per_half = FFN_CHUNK // LANES
    for c in range(n_chunks):
        cols = slice(c * 2 * FFN_CHUNK, (c + 1) * 2 * FFN_CHUNK)
        h = _dot(xb, wup_ref[:, cols]) + bup_ref[:, cols]
        ys = []
        for q in range(2 * per_half):
            lc = c * 2 * per_half + q
            lanes = slice(lc * LANES, (lc + 1) * LANES)
            hq = h[:, q * LANES:(q + 1) * LANES]
            hbuf[lc, FFN_HALO:FFN_HALO + tm, :] = hq
            h1 = hbuf[lc, pl.ds(FFN_HALO - 1, tm, stride=1), :]
            h2 = hbuf[lc, pl.ds(FFN_HALO - 2, tm, stride=1), :]
            ys.append(wdw_ref[2:3, lanes] * hq + wdw_ref[1:2, lanes] * h1
                      + wdw_ref[0:1, lanes] * h2 + bdw_ref[:, lanes])
        acts = [ys[q] * jax.nn.sigmoid(ys[q]) * ys[per_half + q] for q in range(per_half)]
        a = jnp.concatenate(acts, axis=1).astype(_BF16)
        part = _dot(a, wdn_ref[c * FFN_CHUNK:(c + 1) * FFN_CHUNK, :])
        if c == 0:
            acc_ref[...] = part
        else:
            acc_ref[...] += part
    hbuf[:, 0:FFN_HALO, :] = hbuf[:, tm:tm + FFN_HALO, :]
    z = ALPHA * x + (acc_ref[...] + bdn_ref[...])
    o_ref[0] = _layer_norm(z, g_ref[...], b_ref[...])


def _ffn_layer(x, wup, bup, wdw, bdw, wdn, bdn, g, b):
    bsz, t, d = x.shape
    tm = TIME_TILE
    return pl.pallas_call(
        _ffn_kernel,
        grid=(bsz, t // tm),
        in_specs=[_tile_spec(tm), _resident(wup.shape), _resident(bup.shape),
                  _resident(wdw.shape), _resident(bdw.shape), _resident(wdn.shape),
                  _resident(bdn.shape), _resident(g.shape), _resident(b.shape)],
        out_specs=_tile_spec(tm),
        out_shape=jax.ShapeDtypeStruct(x.shape, x.dtype),
        scratch_shapes=[pltpu.VMEM((2 * FFN_HIDDEN // LANES, FFN_HALO + tm, LANES), _F32),
                        pltpu.VMEM((tm, d), _F32)],
        compiler_params=_params(),
        name="conv_ffn",
    )(x, wup, bup, wdw, bdw, wdn, bdn, g, b)


def _conformer_kernel(x_ref, win_ref, bin_ref, wdw_ref, bdw_ref, lng_ref, lnb_ref,
                      wout_ref, bout_ref, g_ref, b_ref, o_ref, gbuf, cbuf):
    tm = x_ref.shape[1]
    c = D_MODEL
    j = pl.program_id(1)

    @pl.when(j == 0)
    def _():
        gbuf[:, 0:CONV_HALO, :] = jnp.zeros((gbuf.shape[0], CONV_HALO, LANES), _F32)

    x = x_ref[0]
    h = _dot(x.astype(_BF16), win_ref[...]) + bin_ref[...]
    for lc in range(c // LANES):
        a = h[:, lc * LANES:(lc + 1) * LANES]
        gate = h[:, c + lc * LANES:c + (lc + 1) * LANES]
        gbuf[lc, CONV_HALO:CONV_HALO + tm, :] = a * jax.nn.sigmoid(gate)

    for lc in range(c // LANES):
        lanes = slice(lc * LANES, (lc + 1) * LANES)
        bias = jnp.broadcast_to(bdw_ref[:, lanes], (CONV_ROWS, LANES))

        def row_block(i, carry, lc=lc, lanes=lanes, bias=bias):
            r0 = pl.multiple_of(i * CONV_ROWS, CONV_ROWS)
            acc = bias
            for k in range(CONV_KERNEL):
                start = r0 + (CONV_HALO - (CONV_KERNEL - 1) + k)
                acc = acc + wdw_ref[k:k + 1, lanes] * gbuf[lc, pl.ds(start, CONV_ROWS, stride=1), :]
            cbuf[pl.ds(r0, CONV_ROWS), lanes] = acc
            return carry

        lax.fori_loop(0, tm // CONV_ROWS, row_block, 0)

    gbuf[:, 0:CONV_HALO, :] = gbuf[:, tm:tm + CONV_HALO, :]
    u = _layer_norm(cbuf[...], lng_ref[...], lnb_ref[...])
    s = (u * jax.nn.sigmoid(u)).astype(_BF16)
    y = _dot(s, wout_ref[...]) + bout_ref[...]
    z = ALPHA * x + y
    o_ref[0] = _layer_norm(z, g_ref[...], b_ref[...])


def _conformer_layer(x, win, bin_, wdw, bdw, lng, lnb, wout, bout, g, b):
    bsz, t, d = x.shape
    tm = TIME_TILE
    args = (x, win, bin_, wdw, bdw, lng, lnb, wout, bout, g, b)
    return pl.pallas_call(
        _conformer_kernel,
        grid=(bsz, t // tm),
        in_specs=[_tile_spec(tm)] + [_resident(a.shape) for a in args[1:]],
        out_specs=_tile_spec(tm),
        out_shape=jax.ShapeDtypeStruct(x.shape, x.dtype),
        scratch_shapes=[pltpu.VMEM((d // LANES, CONV_HALO + tm, LANES), _F32),
                        pltpu.VMEM((tm, d), _F32)],
        compiler_params=_params(),
        name="conformer_conv",
    )(*args)


def _gmlp_kernel(x_ref, win_ref, bin_ref, lng_ref, lnb_ref, ws_ref, bs_ref,
                 wout_ref, bout_ref, g_ref, b_ref, o_ref, ubuf, vbuf, sbuf):
    tm = x_ref.shape[1]
    half = D_MODEL
    L = GMLP_CHUNK
    x = x_ref[0]
    h = _dot(x.astype(_BF16), win_ref[...]) + bin_ref[...]
    z = 0.5 * h * (1.0 + lax.erf(h * (0.5 ** 0.5)))
    ubuf[...] = z[:, :half]
    vbuf[...] = _layer_norm(z[:, half:], lng_ref[...], lnb_ref[...]).astype(_BF16)

    row = lax.broadcasted_iota(jnp.int32, (L, L), 0)
    col = lax.broadcasted_iota(jnp.int32, (L, L), 1)
    causal = col <= row
    for g in range(GMLP_GROUPS):
        cols = slice(g * L, (g + 1) * L)
        w = jnp.where(causal, ws_ref[g], 0.0).astype(_BF16)
        bias = jnp.broadcast_to(bs_ref[:, g:g + 1], (L, L))
        for n in range(tm // L):
            rows = slice(n * L, (n + 1) * L)
            s = _dot(w, vbuf[rows, cols]) + bias
            sbuf[rows, cols] = (ubuf[rows, cols] * s).astype(_BF16)
    y = _dot(sbuf[...], wout_ref[...]) + bout_ref[...]
    zres = ALPHA * x + y
    o_ref[0] = _layer_norm(zres, g_ref[...], b_ref[...])


def _gmlp_layer(x, win, bin_, lng, lnb, ws, bs_col, wout, bout, g, b):
    bsz, t, d = x.shape
    tm = TIME_TILE
    args = (x, win, bin_, lng, lnb, ws, bs_col, wout, bout, g, b)
    return pl.pallas_call(
        _gmlp_kernel,
        grid=(bsz, t // tm),
        in_specs=[_tile_spec(tm)] + [_resident(a.shape) for a in args[1:]],
        out_specs=_tile_spec(tm),
        out_shape=jax.ShapeDtypeStruct(x.shape, x.dtype),
        scratch_shapes=[pltpu.VMEM((tm, d), _F32),
                        pltpu.VMEM((tm, d), _BF16),
                        pltpu.VMEM((tm, d), _BF16)],
        compiler_params=_params(),
        name="gmlp_gating",
    )(*args)


def _row(v):
    return v.reshape(1, -1)


def _interleave_halves(a):
    lead = a.shape[:-1]
    n_chunks = FFN_HIDDEN // FFN_CHUNK
    a = a.reshape(lead + (2, n_chunks, FFN_CHUNK))
    a = jnp.swapaxes(a, -3, -2)
    return a.reshape(lead + (2 * FFN_HIDDEN,))


def kernel(x, conv_w_in, conv_b_in, conv_w_dw, conv_b_dw, conv_ln_g, conv_ln_b, conv_w_out, conv_b_out, gmlp_w_in, gmlp_b_in, gmlp_ln_g, gmlp_ln_b, gmlp_w_s, gmlp_b_s, gmlp_w_out, gmlp_b_out, ffn_w_up, ffn_b_up, ffn_w_dw, ffn_b_dw, ffn_w_down, ffn_b_down, norm1_g, norm1_b, norm2_g, norm2_b):
    for i in range(DEPTH):
        j = i // 2
        if i % 2 == 0:
            x = _conformer_layer(
                x, conv_w_in[j].astype(_BF16), _row(conv_b_in[j]), conv_w_dw[j],
                _row(conv_b_dw[j]), _row(conv_ln_g[j]), _row(conv_ln_b[j]),
                conv_w_out[j].astype(_BF16), _row(conv_b_out[j]),
                _row(norm1_g[i]), _row(norm1_b[i]))
        else:
            x = _gmlp_layer(
                x, gmlp_w_in[j].astype(_BF16), _row(gmlp_b_in[j]), _row(gmlp_ln_g[j]),
                _row(gmlp_ln_b[j]), gmlp_w_s[j], gmlp_b_s[j].T,
                gmlp_w_out[j].astype(_BF16), _row(gmlp_b_out[j]),
                _row(norm1_g[i]), _row(norm1_b[i]))
        x = _ffn_layer(
            x, _interleave_halves(ffn_w_up[i]).astype(_BF16),
            _row(_interleave_halves(ffn_b_up[i])), _interleave_halves(ffn_w_dw[i]),
            _row(_interleave_halves(ffn_b_dw[i])), ffn_w_down[i].astype(_BF16),
            _row(ffn_b_down[i]), _row(norm2_g[i]), _row(norm2_b[i]))
    return x
```

```python
import functools

import jax
import jax.numpy as jnp
from jax import lax
from jax.experimental import pallas as pl
from jax.experimental.pallas import tpu as pltpu

D_MODEL = 1024
DEPTH = 4
CONV_KERNEL = 31
GMLP_CHUNK = 128
GMLP_GROUPS = 8
FFN_HIDDEN = 2816
FFN_CONV = 3
ALPHA = (2.0 * DEPTH) ** 0.25
LN_EPS = 1e-5

TIME_TILE = 512
FFN_CHUNK = 256
CONV_HALO = 32
FFN_HALO = 8
CONV_ROWS = 64
LANES = 128
VMEM_LIMIT_BYTES = 56 * 1024 * 1024

_BF16 = jnp.bfloat16
_F32 = jnp.float32


def _layer_norm(x, g, b):
    mu = jnp.mean(x, axis=-1, keepdims=True)
    xc = x - mu
    var = jnp.mean(xc * xc, axis=-1, keepdims=True)
    return xc * lax.rsqrt(var + LN_EPS) * g + b


def _dot(a, b):
    return jnp.dot(a, b, preferred_element_type=_F32)


def _resident(shape):
    nd = len(shape)
    return pl.BlockSpec(shape, lambda b, j: (0,) * nd, pipeline_mode=pl.Buffered(1))


def _tile_spec(tm):
    return pl.BlockSpec((1, tm, D_MODEL), lambda b, j: (b, j, 0))


def _params():
    return pltpu.CompilerParams(
        dimension_semantics=("arbitrary", "arbitrary"),
        vmem_limit_bytes=VMEM_LIMIT_BYTES,
    )


def _ffn_kernel(x_ref, wup_ref, bup_ref, wdw_ref, bdw_ref, wdn_ref, bdn_ref,
                g_ref, b_ref, o_ref, hbuf, abuf):
    tm = x_ref.shape[1]
    n_chunks = FFN_HIDDEN // FFN_CHUNK
    j = pl.program_id(1)

    @pl.when(j == 0)
    def _():
        hbuf[:, 0:FFN_HALO, :] = jnp.zeros((hbuf.shape[0], FFN_HALO, LANES), _F32)

    x = x_ref[0]
    xb = x.astype(_BF16)
    per_half = FFN_CHUNK // LANES
    for c in range(n_chunks):
        ys = []
        for half in range(2):
            col0 = half * FFN_HIDDEN + c * FFN_CHUNK
            cols = slice(col0, col0 + FFN_CHUNK)
            h = _dot(xb, wup_ref[:, cols]) + bup_ref[:, cols]
            for q in range(per_half):
                lc = col0 // LANES + q
                lanes = slice(lc * LANES, (lc + 1) * LANES)
                hq = h[:, q * LANES:(q + 1) * LANES]
                hbuf[lc, FFN_HALO:FFN_HALO + tm, :] = hq
                h1 = hbuf[lc, pl.ds(FFN_HALO - 1, tm, stride=1), :]
                h2 = hbuf[lc, pl.ds(FFN_HALO - 2, tm, stride=1), :]
                ys.append(wdw_ref[2:3, lanes] * hq + wdw_ref[1:2, lanes] * h1
                          + wdw_ref[0:1, lanes] * h2 + bdw_ref[:, lanes])
        acts = [ys[q] * jax.nn.sigmoid(ys[q]) * ys[per_half + q] for q in range(per_half)]
        abuf[:, c * FFN_CHUNK:(c + 1) * FFN_CHUNK] = jnp.concatenate(acts, axis=1).astype(_BF16)
    hbuf[:, 0:FFN_HALO, :] = hbuf[:, tm:tm + FFN_HALO, :]
    y = _dot(abuf[...], wdn_ref[...]) + bdn_ref[...]
    o_ref[0] = _layer_norm(ALPHA * x + y, g_ref[...], b_ref[...])


def _ffn_layer(x, wup, bup, wdw, bdw, wdn, bdn, g, b):
    bsz, t, d = x.shape
    tm = TIME_TILE
    return pl.pallas_call(
        _ffn_kernel,
        grid=(bsz, t // tm),
        in_specs=[_tile_spec(tm), _resident(wup.shape), _resident(bup.shape),
                  _resident(wdw.shape), _resident(bdw.shape), _resident(wdn.shape),
                  _resident(bdn.shape), _resident(g.shape), _resident(b.shape)],
        out_specs=_tile_spec(tm),
        out_shape=jax.ShapeDtypeStruct(x.shape, x.dtype),
        scratch_shapes=[pltpu.VMEM((2 * FFN_HIDDEN // LANES, FFN_HALO + tm, LANES), _F32),
                        pltpu.VMEM((tm, FFN_HIDDEN), _BF16)],
        compiler_params=_params(),
        name="conv_ffn",
    )(x, wup, bup, wdw, bdw, wdn, bdn, g, b)


def _conformer_kernel(x_ref, win_ref, bin_ref, wdw_ref, bdw_ref, lng_ref, lnb_ref,
                      wout_ref, bout_ref, g_ref, b_ref, o_ref, gbuf, cbuf):
    tm = x_ref.shape[1]
    c = D_MODEL
    j = pl.program_id(1)

    @pl.when(j == 0)
    def _():
        gbuf[:, 0:CONV_HALO, :] = jnp.zeros((gbuf.shape[0], CONV_HALO, LANES), _F32)

    x = x_ref[0]
    h = _dot(x.astype(_BF16), win_ref[...]) + bin_ref[...]
    for lc in range(c // LANES):
        a = h[:, lc * LANES:(lc + 1) * LANES]
        gate = h[:, c + lc * LANES:c + (lc + 1) * LANES]
        gbuf[lc, CONV_HALO:CONV_HALO + tm, :] = a * jax.nn.sigmoid(gate)

    for lc in range(c // LANES):
        lanes = slice(lc * LANES, (lc + 1) * LANES)
        bias = jnp.broadcast_to(bdw_ref[:, lanes], (CONV_ROWS, LANES))

        def row_block(i, carry, lc=lc, lanes=lanes, bias=bias):
            r0 = pl.multiple_of(i * CONV_ROWS, CONV_ROWS)
            acc = bias
            for k in range(CONV_KERNEL):
                start = r0 + (CONV_HALO - (CONV_KERNEL - 1) + k)
                acc = acc + wdw_ref[k:k + 1, lanes] * gbuf[lc, pl.ds(start, CONV_ROWS, stride=1), :]
            cbuf[pl.ds(r0, CONV_ROWS), lanes] = acc
            return carry

        lax.fori_loop(0, tm // CONV_ROWS, row_block, 0)

    gbuf[:, 0:CONV_HALO, :] = gbuf[:, tm:tm + CONV_HALO, :]
    u = _layer_norm(cbuf[...], lng_ref[...], lnb_ref[...])
    s = (u * jax.nn.sigmoid(u)).astype(_BF16)
    y = _dot(s, wout_ref[...]) + bout_ref[...]
    z = ALPHA * x + y
    o_ref[0] = _layer_norm(z, g_ref[...], b_ref[...])


def _conformer_layer(x, win, bin_, wdw, bdw, lng, lnb, wout, bout, g, b):
    bsz, t, d = x.shape
    tm = TIME_TILE
    args = (x, win, bin_, wdw, bdw, lng, lnb, wout, bout, g, b)
    return pl.pallas_call(
        _conformer_kernel,
        grid=(bsz, t // tm),
        in_specs=[_tile_spec(tm)] + [_resident(a.shape) for a in args[1:]],
        out_specs=_tile_spec(tm),
        out_shape=jax.ShapeDtypeStruct(x.shape, x.dtype),
        scratch_shapes=[pltpu.VMEM((d // LANES, CONV_HALO + tm, LANES), _F32),
                        pltpu.VMEM((tm, d), _F32)],
        compiler_params=_params(),
        name="conformer_conv",
    )(*args)


def _gmlp_kernel(x_ref, win_ref, bin_ref, lng_ref, lnb_ref, ws_ref, bs_ref,
                 wout_ref, bout_ref, g_ref, b_ref, o_ref, ubuf, vbuf, sbuf):
    tm = x_ref.shape[1]
    half = D_MODEL
    L = GMLP_CHUNK
    x = x_ref[0]
    h = _dot(x.astype(_BF16), win_ref[...]) + bin_ref[...]
    z = 0.5 * h * (1.0 + lax.erf(h * (0.5 ** 0.5)))
    ubuf[...] = z[:, :half]
    vbuf[...] = _layer_norm(z[:, half:], lng_ref[...], lnb_ref[...]).astype(_BF16)

    row = lax.broadcasted_iota(jnp.int32, (L, L), 0)
    col = lax.broadcasted_iota(jnp.int32, (L, L), 1)
    causal = col <= row
    for g in range(GMLP_GROUPS):
        cols = slice(g * L, (g + 1) * L)
        w = jnp.where(causal, ws_ref[g], 0.0).astype(_BF16)
        bias = jnp.broadcast_to(bs_ref[:, g:g + 1], (L, L))
        for n in range(tm // L):
            rows = slice(n * L, (n + 1) * L)
            s = _dot(w, vbuf[rows, cols]) + bias
            sbuf[rows, cols] = (ubuf[rows, cols] * s).astype(_BF16)
    y = _dot(sbuf[...], wout_ref[...]) + bout_ref[...]
    zres = ALPHA * x + y
    o_ref[0] = _layer_norm(zres, g_ref[...], b_ref[...])


def _gmlp_layer(x, win, bin_, lng, lnb, ws, bs_col, wout, bout, g, b):
    bsz, t, d = x.shape
    tm = TIME_TILE
    args = (x, win, bin_, lng, lnb, ws, bs_col, wout, bout, g, b)
    return pl.pallas_call(
        _gmlp_kernel,
        grid=(bsz, t // tm),
        in_specs=[_tile_spec(tm)] + [_resident(a.shape) for a in args[1:]],
        out_specs=_tile_spec(tm),
        out_shape=jax.ShapeDtypeStruct(x.shape, x.dtype),
        scratch_shapes=[pltpu.VMEM((tm, d), _F32),
                        pltpu.VMEM((tm, d), _BF16),
                        pltpu.VMEM((tm, d), _BF16)],
        compiler_params=_params(),
        name="gmlp_gating",
    )(*args)


def _row(v):
    return v.reshape(1, -1)


def kernel(x, conv_w_in, conv_b_in, conv_w_dw, conv_b_dw, conv_ln_g, conv_ln_b, conv_w_out, conv_b_out, gmlp_w_in, gmlp_b_in, gmlp_ln_g, gmlp_ln_b, gmlp_w_s, gmlp_b_s, gmlp_w_out, gmlp_b_out, ffn_w_up, ffn_b_up, ffn_w_dw, ffn_b_dw, ffn_w_down, ffn_b_down, norm1_g, norm1_b, norm2_g, norm2_b):
    for i in range(DEPTH):
        j = i // 2
        if i % 2 == 0:
            x = _conformer_layer(
                x, conv_w_in[j].astype(_BF16), _row(conv_b_in[j]), conv_w_dw[j],
                _row(conv_b_dw[j]), _row(conv_ln_g[j]), _row(conv_ln_b[j]),
                conv_w_out[j].astype(_BF16), _row(conv_b_out[j]),
                _row(norm1_g[i]), _row(norm1_b[i]))
        else:
            x = _gmlp_layer(
                x, gmlp_w_in[j].astype(_BF16), _row(gmlp_b_in[j]), _row(gmlp_ln_g[j]),
                _row(gmlp_ln_b[j]), gmlp_w_s[j], gmlp_b_s[j].T,
                gmlp_w_out[j].astype(_BF16), _row(gmlp_b_out[j]),
                _row(norm1_g[i]), _row(norm1_b[i]))
        x = _ffn_layer(
            x, ffn_w_up[i].astype(_BF16), _row(ffn_b_up[i]), ffn_w_dw[i],
            _row(ffn_b_dw[i]), ffn_w_down[i].astype(_BF16),
            _row(ffn_b_down[i]), _row(norm2_g[i]), _row(norm2_b[i]))
    return x
```

```python
import jax
import jax.numpy as jnp
from jax import lax
from jax.experimental import pallas as pl
from jax.experimental.pallas import tpu as pltpu

D_MODEL = 1024
DEPTH = 4
CONV_KERNEL = 31
GMLP_CHUNK = 128
GMLP_GROUPS = 8
FFN_HIDDEN = 2816
ALPHA = (2.0 * DEPTH) ** 0.25
LN_EPS = 1e-5

TIME_TILE = 512
GMLP_TILE = 1024
GMLP_PARTS = 4
OUT_PARTS = 2
FFN_CHUNK = 256
CONV_HALO = 32
FFN_HALO = 8
CONV_ROWS = 64
LANES = 128
VMEM_LIMIT_BYTES = 56 * 1024 * 1024

_BF16 = jnp.bfloat16
_F32 = jnp.float32


def _layer_norm(x, g, b):
    mu = jnp.mean(x, axis=-1, keepdims=True)
    xc = x - mu
    var = jnp.mean(xc * xc, axis=-1, keepdims=True)
    return xc * lax.rsqrt(var + LN_EPS) * g + b


def _dot(a, b):
    return jnp.dot(a, b, preferred_element_type=_F32)


def _row_parts(n_rows, n_parts):
    step = n_rows // n_parts
    return [slice(p * step, (p + 1) * step) for p in range(n_parts)]


def _resident(shape):
    nd = len(shape)
    return pl.BlockSpec(shape, lambda b, j: (0,) * nd, pipeline_mode=pl.Buffered(1))


def _tile_spec(tm):
    return pl.BlockSpec((1, tm, D_MODEL), lambda b, j: (b, j, 0))


def _params():
    return pltpu.CompilerParams(
        dimension_semantics=("arbitrary", "arbitrary"),
        vmem_limit_bytes=VMEM_LIMIT_BYTES,
    )


def _ffn_kernel(x_ref, wup_ref, bup_ref, wdw_ref, bdw_ref, wdn_ref, bdn_ref,
                g_ref, b_ref, o_ref, hbuf, abuf):
    tm = x_ref.shape[1]
    n_chunks = FFN_HIDDEN // FFN_CHUNK
    j = pl.program_id(1)

    @pl.when(j == 0)
    def _():
        hbuf[:, 0:FFN_HALO, :] = jnp.zeros((hbuf.shape[0], FFN_HALO, LANES), _F32)

    xb = x_ref[0].astype(_BF16)
    per_half = FFN_CHUNK // LANES
    for c in range(n_chunks):
        ys = []
        for half in range(2):
            col0 = half * FFN_HIDDEN + c * FFN_CHUNK
            cols = slice(col0, col0 + FFN_CHUNK)
            h = _dot(xb, wup_ref[:, cols]) + bup_ref[:, cols]
            for q in range(per_half):
                lc = col0 // LANES + q
                lanes = slice(lc * LANES, (lc + 1) * LANES)
                hq = h[:, q * LANES:(q + 1) * LANES]
                hbuf[lc, FFN_HALO:FFN_HALO + tm, :] = hq
                h1 = hbuf[lc, pl.ds(FFN_HALO - 1, tm, stride=1), :]
                h2 = hbuf[lc, pl.ds(FFN_HALO - 2, tm, stride=1), :]
                ys.append(wdw_ref[2:3, lanes] * hq + wdw_ref[1:2, lanes] * h1
                          + wdw_ref[0:1, lanes] * h2 + bdw_ref[:, lanes])
        acts = [ys[q] * jax.nn.sigmoid(ys[q]) * ys[per_half + q] for q in range(per_half)]
        abuf[:, c * FFN_CHUNK:(c + 1) * FFN_CHUNK] = jnp.concatenate(acts, axis=1).astype(_BF16)
    hbuf[:, 0:FFN_HALO, :] = hbuf[:, tm:tm + FFN_HALO, :]
    for rows in _row_parts(tm, OUT_PARTS):
        y = _dot(abuf[rows, :], wdn_ref[...]) + bdn_ref[...]
        o_ref[0, rows, :] = _layer_norm(ALPHA * x_ref[0, rows, :] + y, g_ref[...], b_ref[...])


def _ffn_layer(x, wup, bup, wdw, bdw, wdn, bdn, g, b):
    bsz, t, d = x.shape
    tm = TIME_TILE
    return pl.pallas_call(
        _ffn_kernel,
        grid=(bsz, t // tm),
        in_specs=[_tile_spec(tm), _resident(wup.shape), _resident(bup.shape),
                  _resident(wdw.shape), _resident(bdw.shape), _resident(wdn.shape),
                  _resident(bdn.shape), _resident(g.shape), _resident(b.shape)],
        out_specs=_tile_spec(tm),
        out_shape=jax.ShapeDtypeStruct(x.shape, x.dtype),
        scratch_shapes=[pltpu.VMEM((2 * FFN_HIDDEN // LANES, FFN_HALO + tm, LANES), _F32),
                        pltpu.VMEM((tm, FFN_HIDDEN), _BF16)],
        compiler_params=_params(),
        name="conv_ffn",
    )(x, wup, bup, wdw, bdw, wdn, bdn, g, b)


def _conformer_kernel(x_ref, win_ref, bin_ref, wdw_ref, bdw_ref, lng_ref, lnb_ref,
                      wout_ref, bout_ref, g_ref, b_ref, o_ref, gbuf, sbuf):
    tm = x_ref.shape[1]
    c = D_MODEL
    n_lane_chunks = c // LANES
    j = pl.program_id(1)

    @pl.when(j == 0)
    def _():
        gbuf[:, 0:CONV_HALO, :] = jnp.zeros((n_lane_chunks, CONV_HALO, LANES), _F32)

    def glu_part(rows):
        h = _dot(x_ref[0, rows, :].astype(_BF16), win_ref[...]) + bin_ref[...]
        for lc in range(n_lane_chunks):
            a = h[:, lc * LANES:(lc + 1) * LANES]
            gate = h[:, c + lc * LANES:c + (lc + 1) * LANES]
            gbuf[lc, CONV_HALO + rows.start:CONV_HALO + rows.stop, :] = a * jax.nn.sigmoid(gate)

    def conv_part(rows):
        outs = []
        for lc in range(n_lane_chunks):
            lanes = slice(lc * LANES, (lc + 1) * LANES)
            blocks = []
            for r0 in range(rows.start, rows.stop, CONV_ROWS):
                acc = jnp.broadcast_to(bdw_ref[:, lanes], (CONV_ROWS, LANES))
                for k in range(CONV_KERNEL):
                    start = r0 + (CONV_HALO - (CONV_KERNEL - 1) + k)
                    acc = acc + wdw_ref[k:k + 1, lanes] * gbuf[lc, pl.ds(start, CONV_ROWS, stride=1), :]
                blocks.append(acc)
            outs.append(jnp.concatenate(blocks, axis=0))
        u = _layer_norm(jnp.concatenate(outs, axis=1), lng_ref[...], lnb_ref[...])
        sbuf[rows, :] = (u * jax.nn.sigmoid(u)).astype(_BF16)

    def out_part(rows):
        y = _dot(sbuf[rows, :], wout_ref[...]) + bout_ref[...]
        o_ref[0, rows, :] = _layer_norm(ALPHA * x_ref[0, rows, :] + y, g_ref[...], b_ref[...])

    parts = _row_parts(tm, OUT_PARTS)
    for rows in parts:
        glu_part(rows)
    for rows in parts:
        conv_part(rows)
        out_part(rows)
    gbuf[:, 0:CONV_HALO, :] = gbuf[:, tm:tm + CONV_HALO, :]


def _conformer_layer(x, win, bin_, wdw, bdw, lng, lnb, wout, bout, g, b):
    bsz, t, d = x.shape
    tm = TIME_TILE
    args = (x, win, bin_, wdw, bdw, lng, lnb, wout, bout, g, b)
    return pl.pallas_call(
        _conformer_kernel,
        grid=(bsz, t // tm),
        in_specs=[_tile_spec(tm)] + [_resident(a.shape) for a in args[1:]],
        out_specs=_tile_spec(tm),
        out_shape=jax.ShapeDtypeStruct(x.shape, x.dtype),
        scratch_shapes=[pltpu.VMEM((d // LANES, CONV_HALO + tm, LANES), _F32),
                        pltpu.VMEM((tm, d), _BF16)],
        compiler_params=_params(),
        name="conformer_conv",
    )(*args)


def _gmlp_kernel(x_ref, win_ref, bin_ref, lng_ref, lnb_ref, ws_ref, bs_ref,
                 wout_ref, bout_ref, g_ref, b_ref, o_ref, ubuf, vbuf, sbuf, wsb):
    tm = x_ref.shape[1]
    half = D_MODEL
    L = GMLP_CHUNK

    row = lax.broadcasted_iota(jnp.int32, (L, L), 0)
    col = lax.broadcasted_iota(jnp.int32, (L, L), 1)
    for g in range(GMLP_GROUPS):
        wsb[g] = jnp.where(col <= row, ws_ref[g], 0.0).astype(_BF16)

    def in_part(rows):
        h = _dot(x_ref[0, rows, :].astype(_BF16), win_ref[...]) + bin_ref[...]
        z = 0.5 * h * (1.0 + lax.erf(h * (0.5 ** 0.5)))
        ubuf[rows, :] = z[:, :half]
        vbuf[rows, :] = _layer_norm(z[:, half:], lng_ref[...], lnb_ref[...]).astype(_BF16)

    def gate_part(rows):
        for g in range(GMLP_GROUPS):
            cols = slice(g * L, (g + 1) * L)
            bias = jnp.broadcast_to(bs_ref[:, g:g + 1], (L, L))
            for r0 in range(rows.start, rows.stop, L):
                blk = slice(r0, r0 + L)
                s = _dot(wsb[g], vbuf[blk, cols]) + bias
                sbuf[blk, cols] = (ubuf[blk, cols] * s).astype(_BF16)

    def out_part(rows):
        y = _dot(sbuf[rows, :], wout_ref[...]) + bout_ref[...]
        o_ref[0, rows, :] = _layer_norm(ALPHA * x_ref[0, rows, :] + y, g_ref[...], b_ref[...])

    parts = _row_parts(tm, GMLP_PARTS)
    in_part(parts[0])
    for p in range(GMLP_PARTS):
        if p + 1 < GMLP_PARTS:
            in_part(parts[p + 1])
        if p >= 1:
            out_part(parts[p - 1])
        gate_part(parts[p])
    out_part(parts[-1])


def _gmlp_layer(x, win, bin_, lng, lnb, ws, bs_col, wout, bout, g, b):
    bsz, t, d = x.shape
    tm = GMLP_TILE
    args = (x, win, bin_, lng, lnb, ws, bs_col, wout, bout, g, b)
    return pl.pallas_call(
        _gmlp_kernel,
        grid=(bsz, t // tm),
        in_specs=[_tile_spec(tm)] + [_resident(a.shape) for a in args[1:]],
        out_specs=_tile_spec(tm),
        out_shape=jax.ShapeDtypeStruct(x.shape, x.dtype),
        scratch_shapes=[pltpu.VMEM((tm, d), _F32),
                        pltpu.VMEM((tm, d), _BF16),
                        pltpu.VMEM((tm, d), _BF16),
                        pltpu.VMEM((GMLP_GROUPS, GMLP_CHUNK, GMLP_CHUNK), _BF16)],
        compiler_params=_params(),
        name="gmlp_gating",
    )(*args)


def _row(v):
    return v.reshape(1, -1)


def kernel(x, conv_w_in, conv_b_in, conv_w_dw, conv_b_dw, conv_ln_g, conv_ln_b, conv_w_out, conv_b_out, gmlp_w_in, gmlp_b_in, gmlp_ln_g, gmlp_ln_b, gmlp_w_s, gmlp_b_s, gmlp_w_out, gmlp_b_out, ffn_w_up, ffn_b_up, ffn_w_dw, ffn_b_dw, ffn_w_down, ffn_b_down, norm1_g, norm1_b, norm2_g, norm2_b):
    for i in range(DEPTH):
        j = i // 2
        if i % 2 == 0:
            x = _conformer_layer(
                x, conv_w_in[j].astype(_BF16), _row(conv_b_in[j]), conv_w_dw[j],
                _row(conv_b_dw[j]), _row(conv_ln_g[j]), _row(conv_ln_b[j]),
                conv_w_out[j].astype(_BF16), _row(conv_b_out[j]),
                _row(norm1_g[i]), _row(norm1_b[i]))
        else:
            x = _gmlp_layer(
                x, gmlp_w_in[j].astype(_BF16), _row(gmlp_b_in[j]), _row(gmlp_ln_g[j]),
                _row(gmlp_ln_b[j]), gmlp_w_s[j], gmlp_b_s[j].T,
                gmlp_w_out[j].astype(_BF16), _row(gmlp_b_out[j]),
                _row(norm1_g[i]), _row(norm1_b[i]))
        x = _ffn_layer(
            x, ffn_w_up[i].astype(_BF16), _row(ffn_b_up[i]), ffn_w_dw[i],
            _row(ffn_b_dw[i]), ffn_w_down[i].astype(_BF16),
            _row(ffn_b_down[i]), _row(norm2_g[i]), _row(norm2_b[i]))
    return x
```

```python
import jax
import jax.numpy as jnp
from jax import lax
from jax.experimental import pallas as pl
from jax.experimental.pallas import tpu as pltpu

D_MODEL = 1024
DEPTH = 4
CONV_KERNEL = 31
GMLP_CHUNK = 128
GMLP_GROUPS = 8
FFN_HIDDEN = 2816
ALPHA = (2.0 * DEPTH) ** 0.25
LN_EPS = 1e-5

TIME_TILE = 512
GMLP_TILE = 1024
GMLP_PARTS = 4
OUT_PARTS = 2
FFN_CHUNK = 256
CONV_HALO = 32
FFN_HALO = 8
CONV_ROWS = 64
LANES = 128
VMEM_LIMIT_BYTES = 56 * 1024 * 1024

_BF16 = jnp.bfloat16
_F32 = jnp.float32


def _layer_norm(x, g, b):
    mu = jnp.mean(x, axis=-1, keepdims=True)
    xc = x - mu
    var = jnp.mean(xc * xc, axis=-1, keepdims=True)
    return xc * lax.rsqrt(var + LN_EPS) * g + b


def _dot(a, b):
    return jnp.dot(a, b, preferred_element_type=_F32)


def _row_parts(n_rows, n_parts):
    step = n_rows // n_parts
    return [slice(p * step, (p + 1) * step) for p in range(n_parts)]


def _resident(stacked, layer):
    tail = stacked.shape[1:]
    return pl.BlockSpec((None,) + tail, lambda b, j: (layer,) + (0,) * len(tail),
                        pipeline_mode=pl.Buffered(1))


def _tile_spec(tm):
    return pl.BlockSpec((1, tm, D_MODEL), lambda b, j: (b, j, 0))


def _params():
    return pltpu.CompilerParams(
        dimension_semantics=("arbitrary", "arbitrary"),
        vmem_limit_bytes=VMEM_LIMIT_BYTES,
    )


def _ffn_kernel(x_ref, wup_ref, bup_ref, wdw_ref, bdw_ref, wdn_ref, bdn_ref,
                g_ref, b_ref, o_ref, hbuf, abuf):
    tm = x_ref.shape[1]
    n_chunks = FFN_HIDDEN // FFN_CHUNK
    j = pl.program_id(1)

    @pl.when(j == 0)
    def _():
        hbuf[:, 0:FFN_HALO, :] = jnp.zeros((hbuf.shape[0], FFN_HALO, LANES), _F32)

    xb = x_ref[0].astype(_BF16)
    per_half = FFN_CHUNK // LANES
    for c in range(n_chunks):
        hs = []
        for half in range(2):
            col0 = half * FFN_HIDDEN + c * FFN_CHUNK
            cols = slice(col0, col0 + FFN_CHUNK)
            hs.append(_dot(xb, wup_ref[:, cols]) + bup_ref[:, cols])
        ys = []
        for half in range(2):
            col0 = half * FFN_HIDDEN + c * FFN_CHUNK
            for q in range(per_half):
                lc = col0 // LANES + q
                lanes = slice(lc * LANES, (lc + 1) * LANES)
                hq = hs[half][:, q * LANES:(q + 1) * LANES]
                hbuf[lc, FFN_HALO:FFN_HALO + tm, :] = hq
                h1 = hbuf[lc, pl.ds(FFN_HALO - 1, tm, stride=1), :]
                h2 = hbuf[lc, pl.ds(FFN_HALO - 2, tm, stride=1), :]
                ys.append(wdw_ref[2:3, lanes] * hq + wdw_ref[1:2, lanes] * h1
                          + wdw_ref[0:1, lanes] * h2 + bdw_ref[:, lanes])
        acts = [ys[q] * jax.nn.sigmoid(ys[q]) * ys[per_half + q] for q in range(per_half)]
        abuf[:, c * FFN_CHUNK:(c + 1) * FFN_CHUNK] = jnp.concatenate(acts, axis=1).astype(_BF16)
    hbuf[:, 0:FFN_HALO, :] = hbuf[:, tm:tm + FFN_HALO, :]
    for rows in _row_parts(tm, OUT_PARTS):
        y = _dot(abuf[rows, :], wdn_ref[...]) + bdn_ref[...]
        o_ref[0, rows, :] = _layer_norm(ALPHA * x_ref[0, rows, :] + y, g_ref[...], b_ref[...])


def _ffn_layer(x, layer, *params):
    bsz, t, d = x.shape
    tm = TIME_TILE
    return pl.pallas_call(
        _ffn_kernel,
        grid=(bsz, t // tm),
        in_specs=[_tile_spec(tm)] + [_resident(a, layer) for a in params],
        out_specs=_tile_spec(tm),
        out_shape=jax.ShapeDtypeStruct(x.shape, x.dtype),
        scratch_shapes=[pltpu.VMEM((2 * FFN_HIDDEN // LANES, FFN_HALO + tm, LANES), _F32),
                        pltpu.VMEM((tm, FFN_HIDDEN), _BF16)],
        compiler_params=_params(),
        name="conv_ffn",
    )(x, *params)


def _conformer_kernel(x_ref, win_ref, bin_ref, wdw_ref, bdw_ref, lng_ref, lnb_ref,
                      wout_ref, bout_ref, g_ref, b_ref, o_ref, gbuf, sbuf):
    tm = x_ref.shape[1]
    c = D_MODEL
    n_lane_chunks = c // LANES
    j = pl.program_id(1)

    @pl.when(j == 0)
    def _():
        gbuf[:, 0:CONV_HALO, :] = jnp.zeros((n_lane_chunks, CONV_HALO, LANES), _F32)

    def glu_part(rows):
        h = _dot(x_ref[0, rows, :].astype(_BF16), win_ref[...]) + bin_ref[...]
        for lc in range(n_lane_chunks):
            a = h[:, lc * LANES:(lc + 1) * LANES]
            gate = h[:, c + lc * LANES:c + (lc + 1) * LANES]
            gbuf[lc, CONV_HALO + rows.start:CONV_HALO + rows.stop, :] = a * jax.nn.sigmoid(gate)

    def conv_part(rows):
        outs = []
        for lc in range(n_lane_chunks):
            lanes = slice(lc * LANES, (lc + 1) * LANES)
            blocks = []
            for r0 in range(rows.start, rows.stop, CONV_ROWS):
                acc = jnp.broadcast_to(bdw_ref[:, lanes], (CONV_ROWS, LANES))
                for k in range(CONV_KERNEL):
                    start = r0 + (CONV_HALO - (CONV_KERNEL - 1) + k)
                    acc = acc + wdw_ref[k:k + 1, lanes] * gbuf[lc, pl.ds(start, CONV_ROWS, stride=1), :]
                blocks.append(acc)
            outs.append(jnp.concatenate(blocks, axis=0))
        u = _layer_norm(jnp.concatenate(outs, axis=1), lng_ref[...], lnb_ref[...])
        sbuf[rows, :] = (u * jax.nn.sigmoid(u)).astype(_BF16)

    def out_part(rows):
        y = _dot(sbuf[rows, :], wout_ref[...]) + bout_ref[...]
        o_ref[0, rows, :] = _layer_norm(ALPHA * x_ref[0, rows, :] + y, g_ref[...], b_ref[...])

    parts = _row_parts(tm, OUT_PARTS)
    for rows in parts:
        glu_part(rows)
    for rows in parts:
        conv_part(rows)
        out_part(rows)
    gbuf[:, 0:CONV_HALO, :] = gbuf[:, tm:tm + CONV_HALO, :]


def _conformer_layer(x, mixer_layer, norm_layer, *params):
    bsz, t, d = x.shape
    tm = TIME_TILE
    layers = [mixer_layer] * (len(params) - 2) + [norm_layer] * 2
    return pl.pallas_call(
        _conformer_kernel,
        grid=(bsz, t // tm),
        in_specs=[_tile_spec(tm)] + [_resident(a, l) for a, l in zip(params, layers)],
        out_specs=_tile_spec(tm),
        out_shape=jax.ShapeDtypeStruct(x.shape, x.dtype),
        scratch_shapes=[pltpu.VMEM((d // LANES, CONV_HALO + tm, LANES), _F32),
                        pltpu.VMEM((tm, d), _BF16)],
        compiler_params=_params(),
        name="conformer_conv",
    )(x, *params)


def _gmlp_kernel(x_ref, win_ref, bin_ref, lng_ref, lnb_ref, ws_ref, bs_ref,
                 wout_ref, bout_ref, g_ref, b_ref, o_ref, ubuf, vbuf, sbuf, wsb):
    tm = x_ref.shape[1]
    half = D_MODEL
    L = GMLP_CHUNK

    row = lax.broadcasted_iota(jnp.int32, (L, L), 0)
    col = lax.broadcasted_iota(jnp.int32, (L, L), 1)
    for g in range(GMLP_GROUPS):
        wsb[g] = jnp.where(col <= row, ws_ref[g], 0.0).astype(_BF16)

    def in_part(rows):
        h = _dot(x_ref[0, rows, :].astype(_BF16), win_ref[...]) + bin_ref[...]
        z = 0.5 * h * (1.0 + lax.erf(h * (0.5 ** 0.5)))
        ubuf[rows, :] = z[:, :half]
        vbuf[rows, :] = _layer_norm(z[:, half:], lng_ref[...], lnb_ref[...]).astype(_BF16)

    def gate_part(rows):
        for g in range(GMLP_GROUPS):
            cols = slice(g * L, (g + 1) * L)
            bias = jnp.broadcast_to(bs_ref[:, g:g + 1], (L, L))
            for r0 in range(rows.start, rows.stop, L):
                blk = slice(r0, r0 + L)
                s = _dot(wsb[g], vbuf[blk, cols]) + bias
                sbuf[blk, cols] = (ubuf[blk, cols] * s).astype(_BF16)

    def out_part(rows):
        y = _dot(sbuf[rows, :], wout_ref[...]) + bout_ref[...]
        o_ref[0, rows, :] = _layer_norm(ALPHA * x_ref[0, rows, :] + y, g_ref[...], b_ref[...])

    parts = _row_parts(tm, GMLP_PARTS)
    in_part(parts[0])
    for p in range(GMLP_PARTS):
        if p + 1 < GMLP_PARTS:
            in_part(parts[p + 1])
        if p >= 1:
            out_part(parts[p - 1])
        gate_part(parts[p])
    out_part(parts[-1])


def _gmlp_layer(x, mixer_layer, norm_layer, *params):
    bsz, t, d = x.shape
    tm = GMLP_TILE
    layers = [mixer_layer] * (len(params) - 2) + [norm_layer] * 2
    return pl.pallas_call(
        _gmlp_kernel,
        grid=(bsz, t // tm),
        in_specs=[_tile_spec(tm)] + [_resident(a, l) for a, l in zip(params, layers)],
        out_specs=_tile_spec(tm),
        out_shape=jax.ShapeDtypeStruct(x.shape, x.dtype),
        scratch_shapes=[pltpu.VMEM((tm, d), _F32),
                        pltpu.VMEM((tm, d), _BF16),
                        pltpu.VMEM((tm, d), _BF16),
                        pltpu.VMEM((GMLP_GROUPS, GMLP_CHUNK, GMLP_CHUNK), _BF16)],
        compiler_params=_params(),
        name="gmlp_gating",
    )(x, *params)


def _rows(v):
    return v.reshape(v.shape[0], 1, v.shape[1])


def kernel(x, conv_w_in, conv_b_in, conv_w_dw, conv_b_dw, conv_ln_g, conv_ln_b, conv_w_out, conv_b_out, gmlp_w_in, gmlp_b_in, gmlp_ln_g, gmlp_ln_b, gmlp_w_s, gmlp_b_s, gmlp_w_out, gmlp_b_out, ffn_w_up, ffn_b_up, ffn_w_dw, ffn_b_dw, ffn_w_down, ffn_b_down, norm1_g, norm1_b, norm2_g, norm2_b):
    norm1 = (_rows(norm1_g), _rows(norm1_b))
    conf = (conv_w_in.astype(_BF16), _rows(conv_b_in), conv_w_dw, _rows(conv_b_dw),
            _rows(conv_ln_g), _rows(conv_ln_b), conv_w_out.astype(_BF16), _rows(conv_b_out)) + norm1
    gmlp = (gmlp_w_in.astype(_BF16), _rows(gmlp_b_in), _rows(gmlp_ln_g), _rows(gmlp_ln_b),
            gmlp_w_s, jnp.swapaxes(gmlp_b_s, 1, 2), gmlp_w_out.astype(_BF16), _rows(gmlp_b_out)) + norm1
    ffn = (ffn_w_up.astype(_BF16), _rows(ffn_b_up), ffn_w_dw, _rows(ffn_b_dw),
           ffn_w_down.astype(_BF16), _rows(ffn_b_down), _rows(norm2_g), _rows(norm2_b))
    for i in range(DEPTH):
        if i % 2 == 0:
            x = _conformer_layer(x, i // 2, i, *conf)
        else:
            x = _gmlp_layer(x, i // 2, i, *gmlp)
        x = _ffn_layer(x, i, *ffn)
    return x
```

```python
import jax
import jax.numpy as jnp
from jax import lax
from jax.experimental import pallas as pl
from jax.experimental.pallas import tpu as pltpu

D_MODEL = 1024
DEPTH = 4
CONV_KERNEL = 31
GMLP_CHUNK = 128
GMLP_GROUPS = 8
FFN_HIDDEN = 2816
ALPHA = (2.0 * DEPTH) ** 0.25
LN_EPS = 1e-5

TIME_TILE = 512
GMLP_TILE = 1024
GMLP_PARTS = 4
OUT_PARTS = 2
FFN_CHUNK = 256
CONV_HALO = 32
FFN_HALO = 8
CONV_ROWS = 64
CONV_GROUP = 2
LANES = 128
VMEM_LIMIT_BYTES = 56 * 1024 * 1024

_BF16 = jnp.bfloat16
_F32 = jnp.float32


def _layer_norm(x, g, b):
    mu = jnp.mean(x, axis=-1, keepdims=True)
    xc = x - mu
    var = jnp.mean(xc * xc, axis=-1, keepdims=True)
    return xc * lax.rsqrt(var + LN_EPS) * g + b


def _dot(a, b):
    return jnp.dot(a, b, preferred_element_type=_F32)


def _row_parts(n_rows, n_parts):
    step = n_rows // n_parts
    return [slice(p * step, (p + 1) * step) for p in range(n_parts)]


def _resident(stacked, layer):
    tail = stacked.shape[1:]
    return pl.BlockSpec((None,) + tail, lambda b, j: (layer,) + (0,) * len(tail),
                        pipeline_mode=pl.Buffered(1))


def _tile_spec(tm):
    return pl.BlockSpec((1, tm, D_MODEL), lambda b, j: (b, j, 0))


def _params():
    return pltpu.CompilerParams(
        dimension_semantics=("arbitrary", "arbitrary"),
        vmem_limit_bytes=VMEM_LIMIT_BYTES,
    )


def _ffn_kernel(x_ref, wup_ref, bup_ref, wdw_ref, bdw_ref, wdn_ref, bdn_ref,
                g_ref, b_ref, o_ref, hbuf, abuf):
    tm = x_ref.shape[1]
    n_chunks = FFN_HIDDEN // FFN_CHUNK
    j = pl.program_id(1)

    @pl.when(j == 0)
    def _():
        hbuf[:, 0:FFN_HALO, :] = jnp.zeros((hbuf.shape[0], FFN_HALO, LANES), _F32)

    xb = x_ref[0].astype(_BF16)
    per_half = FFN_CHUNK // LANES
    for c in range(n_chunks):
        hs = []
        for half in range(2):
            col0 = half * FFN_HIDDEN + c * FFN_CHUNK
            cols = slice(col0, col0 + FFN_CHUNK)
            hs.append(_dot(xb, wup_ref[:, cols]) + bup_ref[:, cols])
        ys = []
        for half in range(2):
            col0 = half * FFN_HIDDEN + c * FFN_CHUNK
            for q in range(per_half):
                lc = col0 // LANES + q
                lanes = slice(lc * LANES, (lc + 1) * LANES)
                hq = hs[half][:, q * LANES:(q + 1) * LANES]
                hbuf[lc, FFN_HALO:FFN_HALO + tm, :] = hq
                h1 = hbuf[lc, pl.ds(FFN_HALO - 1, tm, stride=1), :]
                h2 = hbuf[lc, pl.ds(FFN_HALO - 2, tm, stride=1), :]
                ys.append(wdw_ref[2:3, lanes] * hq + wdw_ref[1:2, lanes] * h1
                          + wdw_ref[0:1, lanes] * h2 + bdw_ref[:, lanes])
        acts = [ys[q] * jax.nn.sigmoid(ys[q]) * ys[per_half + q] for q in range(per_half)]
        abuf[:, c * FFN_CHUNK:(c + 1) * FFN_CHUNK] = jnp.concatenate(acts, axis=1).astype(_BF16)
    hbuf[:, 0:FFN_HALO, :] = hbuf[:, tm:tm + FFN_HALO, :]
    for rows in _row_parts(tm, OUT_PARTS):
        y = _dot(abuf[rows, :], wdn_ref[...]) + bdn_ref[...]
        o_ref[0, rows, :] = _layer_norm(ALPHA * x_ref[0, rows, :] + y, g_ref[...], b_ref[...])


def _ffn_layer(x, layer, *params):
    bsz, t, d = x.shape
    tm = TIME_TILE
    return pl.pallas_call(
        _ffn_kernel,
        grid=(bsz, t // tm),
        in_specs=[_tile_spec(tm)] + [_resident(a, layer) for a in params],
        out_specs=_tile_spec(tm),
        out_shape=jax.ShapeDtypeStruct(x.shape, x.dtype),
        scratch_shapes=[pltpu.VMEM((2 * FFN_HIDDEN // LANES, FFN_HALO + tm, LANES), _F32),
                        pltpu.VMEM((tm, FFN_HIDDEN), _BF16)],
        compiler_params=_params(),
        name="conv_ffn",
    )(x, *params)


def _conformer_kernel(x_ref, win_ref, bin_ref, wdw_ref, bdw_ref, lng_ref, lnb_ref,
                      wout_ref, bout_ref, g_ref, b_ref, o_ref, gbuf, cbuf, sbuf):
    tm = x_ref.shape[1]
    c = D_MODEL
    n_lane_chunks = c // LANES
    n_groups = n_lane_chunks // CONV_GROUP
    width = CONV_GROUP * LANES
    j = pl.program_id(1)

    @pl.when(j == 0)
    def _():
        gbuf[:, 0:CONV_HALO, :] = jnp.zeros((n_lane_chunks, CONV_HALO, LANES), _F32)

    xb = x_ref[0].astype(_BF16)

    def glu_group(k):
        a_cols = slice(k * width, (k + 1) * width)
        g_cols = slice(c + k * width, c + (k + 1) * width)
        a = _dot(xb, win_ref[:, a_cols]) + bin_ref[:, a_cols]
        gate = _dot(xb, win_ref[:, g_cols]) + bin_ref[:, g_cols]
        glu = a * jax.nn.sigmoid(gate)
        for i in range(CONV_GROUP):
            gbuf[CONV_GROUP * k + i, CONV_HALO:CONV_HALO + tm, :] = glu[:, i * LANES:(i + 1) * LANES]

    def conv_group(k):
        for lc in range(CONV_GROUP * k, CONV_GROUP * (k + 1)):
            lanes = slice(lc * LANES, (lc + 1) * LANES)
            for r0 in range(0, tm, CONV_ROWS):
                acc = jnp.broadcast_to(bdw_ref[:, lanes], (CONV_ROWS, LANES))
                for t in range(CONV_KERNEL):
                    start = r0 + (CONV_HALO - (CONV_KERNEL - 1) + t)
                    acc = acc + wdw_ref[t:t + 1, lanes] * gbuf[lc, pl.ds(start, CONV_ROWS, stride=1), :]
                cbuf[r0:r0 + CONV_ROWS, lanes] = acc
            gbuf[lc, 0:CONV_HALO, :] = gbuf[lc, tm:tm + CONV_HALO, :]

    glu_group(0)
    for k in range(n_groups):
        if k + 1 < n_groups:
            glu_group(k + 1)
        conv_group(k)
    parts = _row_parts(tm, OUT_PARTS)
    for rows in parts:
        u = _layer_norm(cbuf[rows, :], lng_ref[...], lnb_ref[...])
        sbuf[rows, :] = (u * jax.nn.sigmoid(u)).astype(_BF16)
    for rows in parts:
        y = _dot(sbuf[rows, :], wout_ref[...]) + bout_ref[...]
        o_ref[0, rows, :] = _layer_norm(ALPHA * x_ref[0, rows, :] + y, g_ref[...], b_ref[...])


def _conformer_layer(x, mixer_layer, norm_layer, *params):
    bsz, t, d = x.shape
    tm = TIME_TILE
    layers = [mixer_layer] * (len(params) - 2) + [norm_layer] * 2
    return pl.pallas_call(
        _conformer_kernel,
        grid=(bsz, t // tm),
        in_specs=[_tile_spec(tm)] + [_resident(a, l) for a, l in zip(params, layers)],
        out_specs=_tile_spec(tm),
        out_shape=jax.ShapeDtypeStruct(x.shape, x.dtype),
        scratch_shapes=[pltpu.VMEM((d // LANES, CONV_HALO + tm, LANES), _F32),
                        pltpu.VMEM((tm, d), _F32),
                        pltpu.VMEM((tm, d), _BF16)],
        compiler_params=_params(),
        name="conformer_conv",
    )(x, *params)


def _gmlp_kernel(x_ref, win_ref, bin_ref, lng_ref, lnb_ref, ws_ref, bs_ref,
                 wout_ref, bout_ref, g_ref, b_ref, o_ref, ubuf, vbuf, sbuf, wsb):
    tm = x_ref.shape[1]
    half = D_MODEL
    L = GMLP_CHUNK

    row = lax.broadcasted_iota(jnp.int32, (L, L), 0)
    col = lax.broadcasted_iota(jnp.int32, (L, L), 1)
    for g in range(GMLP_GROUPS):
        wsb[g] = jnp.where(col <= row, ws_ref[g], 0.0).astype(_BF16)

    def in_part(rows):
        h = _dot(x_ref[0, rows, :].astype(_BF16), win_ref[...]) + bin_ref[...]
        z = 0.5 * h * (1.0 + lax.erf(h * (0.5 ** 0.5)))
        ubuf[rows, :] = z[:, :half]
        vbuf[rows, :] = _layer_norm(z[:, half:], lng_ref[...], lnb_ref[...]).astype(_BF16)

    def gate_part(rows):
        for g in range(GMLP_GROUPS):
            cols = slice(g * L, (g + 1) * L)
            bias = jnp.broadcast_to(bs_ref[:, g:g + 1], (L, L))
            for r0 in range(rows.start, rows.stop, L):
                blk = slice(r0, r0 + L)
                s = _dot(wsb[g], vbuf[blk, cols]) + bias
                sbuf[blk, cols] = (ubuf[blk, cols] * s).astype(_BF16)

    def out_part(rows):
        y = _dot(sbuf[rows, :], wout_ref[...]) + bout_ref[...]
        o_ref[0, rows, :] = _layer_norm(ALPHA * x_ref[0, rows, :] + y, g_ref[...], b_ref[...])

    parts = _row_parts(tm, GMLP_PARTS)
    in_part(parts[0])
    for p in range(GMLP_PARTS):
        if p + 1 < GMLP_PARTS:
            in_part(parts[p + 1])
        if p >= 1:
            out_part(parts[p - 1])
        gate_part(parts[p])
    out_part(parts[-1])


def _gmlp_layer(x, mixer_layer, norm_layer, *params):
    bsz, t, d = x.shape
    tm = GMLP_TILE
    layers = [mixer_layer] * (len(params) - 2) + [norm_layer] * 2
    return pl.pallas_call(
        _gmlp_kernel,
        grid=(bsz, t // tm),
        in_specs=[_tile_spec(tm)] + [_resident(a, l) for a, l in zip(params, layers)],
        out_specs=_tile_spec(tm),
        out_shape=jax.ShapeDtypeStruct(x.shape, x.dtype),
        scratch_shapes=[pltpu.VMEM((tm, d), _F32),
                        pltpu.VMEM((tm, d), _BF16),
                        pltpu.VMEM((tm, d), _BF16),
                        pltpu.VMEM((GMLP_GROUPS, GMLP_CHUNK, GMLP_CHUNK), _BF16)],
        compiler_params=_params(),
        name="gmlp_gating",
    )(x, *params)


def _rows(v):
    return v.reshape(v.shape[0], 1, v.shape[1])


def kernel(x, conv_w_in, conv_b_in, conv_w_dw, conv_b_dw, conv_ln_g, conv_ln_b, conv_w_out, conv_b_out, gmlp_w_in, gmlp_b_in, gmlp_ln_g, gmlp_ln_b, gmlp_w_s, gmlp_b_s, gmlp_w_out, gmlp_b_out, ffn_w_up, ffn_b_up, ffn_w_dw, ffn_b_dw, ffn_w_down, ffn_b_down, norm1_g, norm1_b, norm2_g, norm2_b):
    norm1 = (_rows(norm1_g), _rows(norm1_b))
    conf = (conv_w_in.astype(_BF16), _rows(conv_b_in), conv_w_dw, _rows(conv_b_dw),
            _rows(conv_ln_g), _rows(conv_ln_b), conv_w_out.astype(_BF16), _rows(conv_b_out)) + norm1
    gmlp = (gmlp_w_in.astype(_BF16), _rows(gmlp_b_in), _rows(gmlp_ln_g), _rows(gmlp_ln_b),
            gmlp_w_s, jnp.swapaxes(gmlp_b_s, 1, 2), gmlp_w_out.astype(_BF16), _rows(gmlp_b_out)) + norm1
    ffn = (ffn_w_up.astype(_BF16), _rows(ffn_b_up), ffn_w_dw, _rows(ffn_b_dw),
           ffn_w_down.astype(_BF16), _rows(ffn_b_down), _rows(norm2_g), _rows(norm2_b))
    for i in range(DEPTH):
        if i % 2 == 0:
            x = _conformer_layer(x, i // 2, i, *conf)
        else:
            x = _gmlp_layer(x, i // 2, i, *gmlp)
        x = _ffn_layer(x, i, *ffn)
    return x
```

```python
import jax
import jax.numpy as jnp
from jax import lax
from jax.experimental import pallas as pl
from jax.experimental.pallas import tpu as pltpu

D_MODEL = 1024
DEPTH = 4
CONV_KERNEL = 31
GMLP_CHUNK = 128
GMLP_GROUPS = 8
FFN_HIDDEN = 2816
ALPHA = (2.0 * DEPTH) ** 0.25
LN_EPS = 1e-5

TIME_TILE = 512
CONF_TILE = 1024
GMLP_TILE = 1024
GMLP_PARTS = 4
OUT_PARTS = 2
FFN_CHUNK = 256
CONV_HALO = 32
FFN_HALO = 8
CONV_ROWS = 64
CONV_GROUP = 2
LANES = 128
VMEM_LIMIT_BYTES = 56 * 1024 * 1024

_BF16 = jnp.bfloat16
_F32 = jnp.float32


def _layer_norm(x, g, b):
    mu = jnp.mean(x, axis=-1, keepdims=True)
    xc = x - mu
    var = jnp.mean(xc * xc, axis=-1, keepdims=True)
    return xc * lax.rsqrt(var + LN_EPS) * g + b


def _dot(a, b):
    return jnp.dot(a, b, preferred_element_type=_F32)


def _row_parts(n_rows, n_parts):
    step = n_rows // n_parts
    return [slice(p * step, (p + 1) * step) for p in range(n_parts)]


def _resident(stacked, layer):
    tail = stacked.shape[1:]
    return pl.BlockSpec((None,) + tail, lambda b, j: (layer,) + (0,) * len(tail),
                        pipeline_mode=pl.Buffered(1))


def _tile_spec(tm):
    return pl.BlockSpec((1, tm, D_MODEL), lambda b, j: (b, j, 0))


def _params():
    return pltpu.CompilerParams(
        dimension_semantics=("arbitrary", "arbitrary"),
        vmem_limit_bytes=VMEM_LIMIT_BYTES,
    )


def _ffn_kernel(x_ref, wup_ref, bup_ref, wdw_ref, bdw_ref, wdn_ref, bdn_ref,
                g_ref, b_ref, o_ref, hbuf, abuf):
    tm = x_ref.shape[1]
    n_chunks = FFN_HIDDEN // FFN_CHUNK
    j = pl.program_id(1)

    @pl.when(j == 0)
    def _():
        hbuf[:, 0:FFN_HALO, :] = jnp.zeros((hbuf.shape[0], FFN_HALO, LANES), _F32)

    xb = x_ref[0].astype(_BF16)
    per_half = FFN_CHUNK // LANES
    for c in range(n_chunks):
        hs = []
        for half in range(2):
            col0 = half * FFN_HIDDEN + c * FFN_CHUNK
            cols = slice(col0, col0 + FFN_CHUNK)
            hs.append(_dot(xb, wup_ref[:, cols]) + bup_ref[:, cols])
        ys = []
        for half in range(2):
            col0 = half * FFN_HIDDEN + c * FFN_CHUNK
            for q in range(per_half):
                lc = col0 // LANES + q
                lanes = slice(lc * LANES, (lc + 1) * LANES)
                hq = hs[half][:, q * LANES:(q + 1) * LANES]
                hbuf[lc, FFN_HALO:FFN_HALO + tm, :] = hq
                h1 = hbuf[lc, pl.ds(FFN_HALO - 1, tm, stride=1), :]
                h2 = hbuf[lc, pl.ds(FFN_HALO - 2, tm, stride=1), :]
                ys.append(wdw_ref[2:3, lanes] * hq + wdw_ref[1:2, lanes] * h1
                          + wdw_ref[0:1, lanes] * h2 + bdw_ref[:, lanes])
        acts = [ys[q] * jax.nn.sigmoid(ys[q]) * ys[per_half + q] for q in range(per_half)]
        abuf[:, c * FFN_CHUNK:(c + 1) * FFN_CHUNK] = jnp.concatenate(acts, axis=1).astype(_BF16)
    hbuf[:, 0:FFN_HALO, :] = hbuf[:, tm:tm + FFN_HALO, :]
    for rows in _row_parts(tm, OUT_PARTS):
        y = _dot(abuf[rows, :], wdn_ref[...]) + bdn_ref[...]
        o_ref[0, rows, :] = _layer_norm(ALPHA * x_ref[0, rows, :] + y, g_ref[...], b_ref[...])


def _ffn_layer(x, layer, *params):
    bsz, t, d = x.shape
    tm = TIME_TILE
    return pl.pallas_call(
        _ffn_kernel,
        grid=(bsz, t // tm),
        in_specs=[_tile_spec(tm)] + [_resident(a, layer) for a in params],
        out_specs=_tile_spec(tm),
        out_shape=jax.ShapeDtypeStruct(x.shape, x.dtype),
        scratch_shapes=[pltpu.VMEM((2 * FFN_HIDDEN // LANES, FFN_HALO + tm, LANES), _F32),
                        pltpu.VMEM((tm, FFN_HIDDEN), _BF16)],
        compiler_params=_params(),
        name="conv_ffn",
    )(x, *params)


def _conformer_kernel(x_ref, win_ref, bin_ref, wdw_ref, bdw_ref, lng_ref, lnb_ref,
                      wout_ref, bout_ref, g_ref, b_ref, o_ref, gbuf, cbuf, sbuf):
    tm = x_ref.shape[1]
    c = D_MODEL
    n_lane_chunks = c // LANES
    n_groups = n_lane_chunks // CONV_GROUP
    width = CONV_GROUP * LANES
    j = pl.program_id(1)

    @pl.when(j == 0)
    def _():
        gbuf[:, 0:CONV_HALO, :] = jnp.zeros((n_lane_chunks, CONV_HALO, LANES), _F32)

    xb = x_ref[0].astype(_BF16)

    def glu_group(k):
        a_cols = slice(k * width, (k + 1) * width)
        g_cols = slice(c + k * width, c + (k + 1) * width)
        a = _dot(xb, win_ref[:, a_cols]) + bin_ref[:, a_cols]
        gate = _dot(xb, win_ref[:, g_cols]) + bin_ref[:, g_cols]
        glu = a * jax.nn.sigmoid(gate)
        for i in range(CONV_GROUP):
            gbuf[CONV_GROUP * k + i, CONV_HALO:CONV_HALO + tm, :] = glu[:, i * LANES:(i + 1) * LANES]

    def conv_group(k):
        for lc in range(CONV_GROUP * k, CONV_GROUP * (k + 1)):
            lanes = slice(lc * LANES, (lc + 1) * LANES)
            for r0 in range(0, tm, CONV_ROWS):
                acc = jnp.broadcast_to(bdw_ref[:, lanes], (CONV_ROWS, LANES))
                for t in range(CONV_KERNEL):
                    start = r0 + (CONV_HALO - (CONV_KERNEL - 1) + t)
                    acc = acc + wdw_ref[t:t + 1, lanes] * gbuf[lc, pl.ds(start, CONV_ROWS, stride=1), :]
                cbuf[r0:r0 + CONV_ROWS, lanes] = acc
            gbuf[lc, 0:CONV_HALO, :] = gbuf[lc, tm:tm + CONV_HALO, :]

    glu_group(0)
    for k in range(n_groups):
        if k + 1 < n_groups:
            glu_group(k + 1)
        conv_group(k)
    parts = _row_parts(tm, OUT_PARTS)
    for rows in parts:
        u = _layer_norm(cbuf[rows, :], lng_ref[...], lnb_ref[...])
        sbuf[rows, :] = (u * jax.nn.sigmoid(u)).astype(_BF16)
    for rows in parts:
        y = _dot(sbuf[rows, :], wout_ref[...]) + bout_ref[...]
        o_ref[0, rows, :] = _layer_norm(ALPHA * x_ref[0, rows, :] + y, g_ref[...], b_ref[...])


def _conformer_layer(x, mixer_layer, norm_layer, *params):
    bsz, t, d = x.shape
    tm = CONF_TILE
    layers = [mixer_layer] * (len(params) - 2) + [norm_layer] * 2
    return pl.pallas_call(
        _conformer_kernel,
        grid=(bsz, t // tm),
        in_specs=[_tile_spec(tm)] + [_resident(a, l) for a, l in zip(params, layers)],
        out_specs=_tile_spec(tm),
        out_shape=jax.ShapeDtypeStruct(x.shape, x.dtype),
        scratch_shapes=[pltpu.VMEM((d // LANES, CONV_HALO + tm, LANES), _F32),
                        pltpu.VMEM((tm, d), _F32),
                        pltpu.VMEM((tm, d), _BF16)],
        compiler_params=_params(),
        name="conformer_conv",
    )(x, *params)


def _gmlp_kernel(x_ref, win_ref, bin_ref, lng_ref, lnb_ref, ws_ref, bs_ref,
                 wout_ref, bout_ref, g_ref, b_ref, o_ref, ubuf, vbuf, sbuf, wsb):
    tm = x_ref.shape[1]
    half = D_MODEL
    L = GMLP_CHUNK

    row = lax.broadcasted_iota(jnp.int32, (L, L), 0)
    col = lax.broadcasted_iota(jnp.int32, (L, L), 1)
    for g in range(GMLP_GROUPS):
        wsb[g] = jnp.where(col <= row, ws_ref[g], 0.0).astype(_BF16)

    def in_part(rows):
        h = _dot(x_ref[0, rows, :].astype(_BF16), win_ref[...]) + bin_ref[...]
        z = 0.5 * h * (1.0 + lax.erf(h * (0.5 ** 0.5)))
        ubuf[rows, :] = z[:, :half]
        vbuf[rows, :] = _layer_norm(z[:, half:], lng_ref[...], lnb_ref[...]).astype(_BF16)

    def gate_part(rows):
        for g in range(GMLP_GROUPS):
            cols = slice(g * L, (g + 1) * L)
            bias = jnp.broadcast_to(bs_ref[:, g:g + 1], (L, L))
            for r0 in range(rows.start, rows.stop, L):
                blk = slice(r0, r0 + L)
                s = _dot(wsb[g], vbuf[blk, cols]) + bias
                sbuf[blk, cols] = (ubuf[blk, cols] * s).astype(_BF16)

    def out_part(rows):
        y = _dot(sbuf[rows, :], wout_ref[...]) + bout_ref[...]
        o_ref[0, rows, :] = _layer_norm(ALPHA * x_ref[0, rows, :] + y, g_ref[...], b_ref[...])

    parts = _row_parts(tm, GMLP_PARTS)
    in_part(parts[0])
    for p in range(GMLP_PARTS):
        if p + 1 < GMLP_PARTS:
            in_part(parts[p + 1])
        if p >= 1:
            out_part(parts[p - 1])
        gate_part(parts[p])
    out_part(parts[-1])


def _gmlp_layer(x, mixer_layer, norm_layer, *params):
    bsz, t, d = x.shape
    tm = GMLP_TILE
    layers = [mixer_layer] * (len(params) - 2) + [norm_layer] * 2
    return pl.pallas_call(
        _gmlp_kernel,
        grid=(bsz, t // tm),
        in_specs=[_tile_spec(tm)] + [_resident(a, l) for a, l in zip(params, layers)],
        out_specs=_tile_spec(tm),
        out_shape=jax.ShapeDtypeStruct(x.shape, x.dtype),
        scratch_shapes=[pltpu.VMEM((tm, d), _F32),
                        pltpu.VMEM((tm, d), _BF16),
                        pltpu.VMEM((tm, d), _BF16),
                        pltpu.VMEM((GMLP_GROUPS, GMLP_CHUNK, GMLP_CHUNK), _BF16)],
        compiler_params=_params(),
        name="gmlp_gating",
    )(x, *params)


def _rows(v):
    return v.reshape(v.shape[0], 1, v.shape[1])


def kernel(x, conv_w_in, conv_b_in, conv_w_dw, conv_b_dw, conv_ln_g, conv_ln_b, conv_w_out, conv_b_out, gmlp_w_in, gmlp_b_in, gmlp_ln_g, gmlp_ln_b, gmlp_w_s, gmlp_b_s, gmlp_w_out, gmlp_b_out, ffn_w_up, ffn_b_up, ffn_w_dw, ffn_b_dw, ffn_w_down, ffn_b_down, norm1_g, norm1_b, norm2_g, norm2_b):
    norm1 = (_rows(norm1_g), _rows(norm1_b))
    conf = (conv_w_in.astype(_BF16), _rows(conv_b_in), conv_w_dw, _rows(conv_b_dw),
            _rows(conv_ln_g), _rows(conv_ln_b), conv_w_out.astype(_BF16), _rows(conv_b_out)) + norm1
    gmlp = (gmlp_w_in.astype(_BF16), _rows(gmlp_b_in), _rows(gmlp_ln_g), _rows(gmlp_ln_b),
            gmlp_w_s, jnp.swapaxes(gmlp_b_s, 1, 2), gmlp_w_out.astype(_BF16), _rows(gmlp_b_out)) + norm1
    ffn = (ffn_w_up.astype(_BF16), _rows(ffn_b_up), ffn_w_dw, _rows(ffn_b_dw),
           ffn_w_down.astype(_BF16), _rows(ffn_b_down), _rows(norm2_g), _rows(norm2_b))
    for i in range(DEPTH):
        if i % 2 == 0:
            x = _conformer_layer(x, i // 2, i, *conf)
        else:
            x = _gmlp_layer(x, i // 2, i, *gmlp)
        x = _ffn_layer(x, i, *ffn)
    return x
```
